```python
import math
import jax, jax.numpy as jnp
from jax import lax
import numpy as np

D_MODEL = 1024
BATCH = 8
SEQ = 4096
DEPTH = 2

N_MEM = 256
CONV_WIDTH = D_MODEL // 2
CONV_K = 31
HGRN_WIDTH = D_MODEL // 2
HGRN_HEAD_DIM = 128
HGRN_HEADS = HGRN_WIDTH // HGRN_HEAD_DIM
CHUNK = 64
XA_HEADS = 4
XA_HEAD_DIM = D_MODEL // XA_HEADS
D_FF = 128 * math.ceil(8 * D_MODEL / 3 / 128)
IN_SIZES = (CONV_WIDTH, CONV_WIDTH, HGRN_WIDTH, HGRN_WIDTH, HGRN_WIDTH, HGRN_WIDTH, D_MODEL, D_MODEL)
IN_COLS = sum(IN_SIZES)
IN_SPLITS = tuple(int(s) for s in np.cumsum(IN_SIZES)[:-1])
DEEPNORM_ALPHA = (2 * DEPTH) ** 0.25
DEEPNORM_BETA = (8 * DEPTH) ** -0.25
LN_EPS = 1e-5
RMS_EPS = 1e-6
KEY_MAX = 1.0 - 1e-6

kernel_name = 'hybrid_conformer_hgrn2_gated_deepnorm'


def layer_norm(x, g, b):
    xf = x.astype(jnp.float32)
    mu = jnp.mean(xf, axis=-1, keepdims=True)
    xc = xf - mu
    var = jnp.mean(xc * xc, axis=-1, keepdims=True)
    y = xc * lax.rsqrt(var + LN_EPS) * g.astype(jnp.float32) + b.astype(jnp.float32)
    return y.astype(x.dtype)


def swiglu(h, w_gu, w_down):
    gate, up = jnp.split(h @ w_gu, 2, axis=-1)
    return (jax.nn.silu(gate) * up) @ w_down


def conformer_conv(a, b_gate, conv_w, conv_b, ln_g, ln_b, w_proj):
    h = a * jax.nn.sigmoid(b_gate)
    h = lax.conv_general_dilated(
        h, conv_w[:, None, :], window_strides=(1,), padding=[(CONV_K - 1, 0)],
        dimension_numbers=('NWC', 'WIO', 'NWC'), feature_group_count=CONV_WIDTH) + conv_b
    h = jax.nn.silu(layer_norm(h, ln_g, ln_b))
    return h @ w_proj


def hgrn2_branch(q, f_logit, i_val, o_gate, lb, norm_g, w_proj):
    bsz, seq = q.shape[0], q.shape[1]
    n_chunks = seq // CHUNK
    z = f_logit.astype(jnp.float32)
    k = (1.0 - lb) * jax.nn.sigmoid(-z)
    log_f = jnp.log1p(-jnp.minimum(k, KEY_MAX))
    qf = jax.nn.silu(q.astype(jnp.float32)) * HGRN_HEAD_DIM ** -0.5
    vf = i_val.astype(jnp.float32)

    def to_chunks(t):
        t = t.reshape(bsz, n_chunks, CHUNK, HGRN_HEADS, HGRN_HEAD_DIM)
        return t.transpose(1, 0, 3, 2, 4)

    causal = jnp.tril(jnp.ones((CHUNK, CHUNK), dtype=bool))[:, :, None]

    def step(state, inp):
        qc, kc, vc, lfc = inp
        b = jnp.cumsum(lfc, axis=2)
        diff = b[:, :, :, None, :] - b[:, :, None, :, :]
        decay = jnp.where(causal, jnp.exp(jnp.where(causal, diff, 0.0)), 0.0)
        scores = jnp.einsum('bhtk,bhtsk,bhsk->bhts', qc, decay, kc)
        o = (jnp.einsum('bhts,bhsv->bhtv', scores, vc)
             + jnp.einsum('bhtk,bhkv->bhtv', qc * jnp.exp(b), state))
        b_last = b[:, :, -1, :]
        new_state = (jnp.exp(b_last)[..., None] * state
                     + jnp.einsum('bhsk,bhsv->bhkv', kc * jnp.exp(b_last[:, :, None, :] - b), vc))
        return new_state, o

    state0 = jnp.zeros((bsz, HGRN_HEADS, HGRN_HEAD_DIM, HGRN_HEAD_DIM), jnp.float32)
    _, o = lax.scan(step, state0, (to_chunks(qf), to_chunks(k), to_chunks(vf), to_chunks(log_f)))
    o = o.transpose(1, 0, 3, 2, 4).reshape(bsz, seq, HGRN_HEADS, HGRN_HEAD_DIM)
    o = o * lax.rsqrt(jnp.mean(o * o, axis=-1, keepdims=True) + RMS_EPS)
    o = o.reshape(bsz, seq, HGRN_WIDTH) * norm_g.astype(jnp.float32) * jax.nn.silu(o_gate.astype(jnp.float32))
    return o.astype(q.dtype) @ w_proj


def cross_attention(h, mem, wq, wkv, wo):
    bsz, seq = h.shape[0], h.shape[1]
    n_mem = mem.shape[1]
    q = (h @ wq).reshape(bsz, seq, XA_HEADS, XA_HEAD_DIM)
    k, v = jnp.split(mem @ wkv, 2, axis=-1)
    k = k.reshape(bsz, n_mem, XA_HEADS, XA_HEAD_DIM)
    v = v.reshape(bsz, n_mem, XA_HEADS, XA_HEAD_DIM)
    s = jnp.einsum('bshd,bmhd->bhsm', q, k).astype(jnp.float32) * XA_HEAD_DIM ** -0.5
    p = jax.nn.softmax(s, axis=-1).astype(h.dtype)
    o = jnp.einsum('bhsm,bmhd->bshd', p, v).reshape(bsz, seq, D_MODEL)
    return o @ wo


def setup_inputs(seed: int = 0) -> dict:
    key = jax.random.key(seed)
    ks = jax.random.split(key, 24)
    f32 = jnp.float32

    def w(k, shape, fan_in, scale=1.0):
        return jax.random.normal(k, shape, f32) * (fan_in ** -0.5) * scale

    wk = w(ks[16], (DEPTH, D_MODEL, D_MODEL), D_MODEL)
    wv = w(ks[17], (DEPTH, D_MODEL, D_MODEL), D_MODEL, DEEPNORM_BETA)
    return {
        'x': jax.random.normal(ks[0], (BATCH, SEQ, D_MODEL), f32),
        'mem': jax.random.normal(ks[1], (BATCH, N_MEM, D_MODEL), f32),
        'ln_g': 1.0 + 0.02 * jax.random.normal(ks[2], (DEPTH, 4, D_MODEL), f32),
        'ln_b': 0.02 * jax.random.normal(ks[3], (DEPTH, 4, D_MODEL), f32),
        'ffn1_w_gu': w(ks[4], (DEPTH, D_MODEL, 2 * D_FF), D_MODEL, DEEPNORM_BETA),
        'ffn1_w_down': w(ks[5], (DEPTH, D_FF, D_MODEL), D_FF, DEEPNORM_BETA),
        'w_in': w(ks[6], (DEPTH, D_MODEL, IN_COLS), D_MODEL),
        'conv_w': w(ks[7], (DEPTH, CONV_K, CONV_WIDTH), CONV_K),
        'conv_b': 0.01 * jax.random.normal(ks[8], (DEPTH, CONV_WIDTH), f32),
        'conv_ln_g': 1.0 + 0.02 * jax.random.normal(ks[9], (DEPTH, CONV_WIDTH), f32),
        'conv_ln_b': 0.02 * jax.random.normal(ks[10], (DEPTH, CONV_WIDTH), f32),
        'w_conv_proj': w(ks[11], (DEPTH, CONV_WIDTH, D_MODEL), CONV_WIDTH, DEEPNORM_BETA),
        'hgrn_lb': 1.0 + 0.1 * jax.random.normal(ks[12], (DEPTH, HGRN_WIDTH), f32),
        'hgrn_norm_g': 1.0 + 0.02 * jax.random.normal(ks[13], (DEPTH, HGRN_WIDTH), f32),
        'w_hgrn_proj': w(ks[14], (DEPTH, HGRN_WIDTH, D_MODEL), HGRN_WIDTH, DEEPNORM_BETA),
        'w_mix_out': w(ks[15], (DEPTH, D_MODEL, D_MODEL), D_MODEL, DEEPNORM_BETA),
        'xa_wq': w(ks[18], (DEPTH, D_MODEL, D_MODEL), D_MODEL),
        'xa_wkv': jnp.concatenate([wk, wv], axis=-1),
        'xa_wo': w(ks[19], (DEPTH, D_MODEL, D_MODEL), D_MODEL, DEEPNORM_BETA),
        'ffn2_w_gu': w(ks[20], (DEPTH, D_MODEL, 2 * D_FF), D_MODEL, DEEPNORM_BETA),
        'ffn2_w_down': w(ks[21], (DEPTH, D_FF, D_MODEL), D_FF, DEEPNORM_BETA),
    }


def reference(x, mem, ln_g, ln_b, ffn1_w_gu, ffn1_w_down, w_in, conv_w, conv_b, conv_ln_g,
              conv_ln_b, w_conv_proj, hgrn_lb, hgrn_norm_g, w_hgrn_proj, w_mix_out,
              xa_wq, xa_wkv, xa_wo, ffn2_w_gu, ffn2_w_down):
    lb_soft = jax.nn.softmax(hgrn_lb.astype(jnp.float32), axis=0)
    lb_all = jnp.cumsum(lb_soft, axis=0) - lb_soft[0]
    for l in range(DEPTH):
        x = layer_norm(DEEPNORM_ALPHA * x + 0.5 * swiglu(x, ffn1_w_gu[l], ffn1_w_down[l]),
                       ln_g[l, 0], ln_b[l, 0])
        a, b_gate, q, f_logit, i_val, o_gate, gate_a, gate_b = jnp.split(x @ w_in[l], IN_SPLITS, axis=-1)
        y_conv = conformer_conv(a, b_gate, conv_w[l], conv_b[l], conv_ln_g[l], conv_ln_b[l], w_conv_proj[l])
        y_hgrn = hgrn2_branch(q, f_logit, i_val, o_gate, lb_all[l], hgrn_norm_g[l], w_hgrn_proj[l])
        mix = (jax.nn.sigmoid(gate_a) * y_conv + jax.nn.sigmoid(gate_b) * y_hgrn) @ w_mix_out[l]
        x = layer_norm(DEEPNORM_ALPHA * x + mix, ln_g[l, 1], ln_b[l, 1])
        x = layer_norm(DEEPNORM_ALPHA * x + cross_attention(x, mem, xa_wq[l], xa_wkv[l], xa_wo[l]),
                       ln_g[l, 2], ln_b[l, 2])
        x = layer_norm(DEEPNORM_ALPHA * x + 0.5 * swiglu(x, ffn2_w_gu[l], ffn2_w_down[l]),
                       ln_g[l, 3], ln_b[l, 3])
    return x
```

```python
import functools
import math

import jax
import jax.numpy as jnp
from jax import lax
from jax.experimental import pallas as pl
from jax.experimental.pallas import tpu as pltpu

F32 = jnp.float32
BF16 = jnp.bfloat16

LN_EPS = 1e-5
RMS_EPS = 1e-6
KEY_MAX = 1.0 - 1e-6
CONV_K = 31
HGRN_HEAD_DIM = 128
XA_HEADS = 4

V7X_SUBLANES = 8
V7X_LANES = 128
V7X_MXU_COLS = 256
V7X_VMEM_LIMIT_BYTES = 60000 * 1024

CONV_HALO = 32
HGRN_CHUNK = 64
HGRN_DIAG = 8


def _cparams(sem):
    return pltpu.CompilerParams(dimension_semantics=sem, vmem_limit_bytes=V7X_VMEM_LIMIT_BYTES)


def _resident(shape):
    nd = len(shape)
    return pl.BlockSpec(shape, lambda *_: (0,) * nd, pipeline_mode=pl.Buffered(1))


def _layer_norm(y, g, b):
    mu = jnp.mean(y, axis=-1, keepdims=True)
    yc = y - mu
    var = jnp.mean(yc * yc, axis=-1, keepdims=True)
    return yc * lax.rsqrt(var + LN_EPS) * g + b


def _dot(a, b):
    return jnp.dot(a, b, preferred_element_type=F32)


def _dot_nt(a, b):
    return lax.dot_general(a, b, (((1,), (1,)), ((), ())), preferred_element_type=F32)


def _dot_tn(a, b):
    return lax.dot_general(a, b, (((0,), (0,)), ((), ())), preferred_element_type=F32)


def _ffn_kernel(x_ref, wgu_ref, wd_ref, g_ref, b_ref, o_ref, *, d_ff, alpha, col_chunks):
    x = x_ref[...]
    xb = x.astype(BF16)
    acc = None
    c0 = 0
    for cw in col_chunks:
        gate = _dot(xb, wgu_ref[:, c0:c0 + cw])
        up = _dot(xb, wgu_ref[:, d_ff + c0:d_ff + c0 + cw])
        act = (gate * jax.nn.sigmoid(gate) * up).astype(BF16)
        part = _dot(act, wd_ref[c0:c0 + cw, :])
        acc = part if acc is None else acc + part
        c0 += cw
    o_ref[...] = _layer_norm(alpha * x + 0.5 * acc, g_ref[...], b_ref[...])


def _ffn_col_chunks(d_ff):
    n_tiles = d_ff // V7X_MXU_COLS
    assert n_tiles * V7X_MXU_COLS == d_ff
    per = 3
    out = [per * V7X_MXU_COLS] * (n_tiles // per)
    if n_tiles % per:
        out.append((n_tiles % per) * V7X_MXU_COLS)
    return tuple(out)


def _ffn_ln(x2d, wgu, wd, g, b, alpha, tm):
    t, d = x2d.shape
    d_ff = wd.shape[0]
    kern = functools.partial(_ffn_kernel, d_ff=d_ff, alpha=alpha, col_chunks=_ffn_col_chunks(d_ff))
    return pl.pallas_call(
        kern,
        grid=(t // tm,),
        in_specs=[
            pl.BlockSpec((tm, d), lambda i: (i, 0)),
            _resident(wgu.shape),
            _resident(wd.shape),
            _resident(g.shape),
            _resident(b.shape),
        ],
        out_specs=pl.BlockSpec((tm, d), lambda i: (i, 0)),
        out_shape=jax.ShapeDtypeStruct((t, d), F32),
        compiler_params=_cparams(("parallel",)),
        name="ffn_ln",
    )(x2d, wgu, wd, g, b)


def _kv_kernel(mem_ref, wkv_ref, k_ref, v_ref, *, d):
    kv = _dot(mem_ref[...].astype(BF16), wkv_ref[...])
    k_ref[...] = kv[:, :d].astype(BF16)
    v_ref[...] = kv[:, d:].astype(BF16)


def _kv_proj(mem, wkv):
    bsz, n_mem, d = mem.shape
    spec = pl.BlockSpec((None, n_mem, d), lambda i: (i, 0, 0))
    return pl.pallas_call(
        functools.partial(_kv_kernel, d=d),
        grid=(bsz,),
        in_specs=[spec, _resident(wkv.shape)],
        out_specs=[spec, spec],
        out_shape=[jax.ShapeDtypeStruct((bsz, n_mem, d), BF16)] * 2,
        compiler_params=_cparams(("parallel",)),
        name="kv_proj",
    )(mem, wkv)


def _xattn_kernel(x_ref, k_ref, v_ref, wq_ref, wo_ref, g_ref, b_ref, o_ref, *, alpha, heads):
    x = x_ref[...]
    d = x.shape[-1]
    hd = d // heads
    q = _dot(x.astype(BF16), wq_ref[...])
    outs = []
    for h in range(heads):
        sl = slice(h * hd, (h + 1) * hd)
        s = _dot_nt(q[:, sl].astype(BF16), k_ref[:, sl]) * (hd ** -0.5)
        e = jnp.exp(s - jnp.max(s, axis=-1, keepdims=True))
        den = jnp.sum(e, axis=-1, keepdims=True)
        outs.append((_dot(e.astype(BF16), v_ref[:, sl]) / den).astype(BF16))
    y = _dot(jnp.concatenate(outs, axis=-1), wo_ref[...])
    o_ref[...] = _layer_norm(alpha * x + y, g_ref[...], b_ref[...])


def _xattn_ln(x, k, v, wq, wo, g, b, alpha, tm):
    bsz, seq, d = x.shape
    n_mem = k.shape[1]
    xspec = pl.BlockSpec((None, tm, d), lambda i, j: (i, j, 0))
    kvspec = pl.BlockSpec((None, n_mem, d), lambda i, j: (i, 0, 0))
    return pl.pallas_call(
        functools.partial(_xattn_kernel, alpha=alpha, heads=XA_HEADS),
        grid=(bsz, seq // tm),
        in_specs=[xspec, kvspec, kvspec, _resident(wq.shape), _resident(wo.shape),
                  _resident(g.shape), _resident(b.shape)],
        out_specs=xspec,
        out_shape=jax.ShapeDtypeStruct(x.shape, F32),
        compiler_params=_cparams(("parallel", "parallel")),
        name="xattn_ln",
    )(x, k, v, wq, wo, g, b)


def _block_ref_rows(b, block, row):
    c, w = b.shape
    b3 = b.reshape(c // block, block, w)
    return jnp.broadcast_to(b3[:, row:row + 1, :], b3.shape).reshape(c, w)


def _score_masks(c):
    t = lax.broadcasted_iota(jnp.int32, (c, c), 0)
    s = lax.broadcasted_iota(jnp.int32, (c, c), 1)
    masks = []
    h = c // 2
    while h >= HGRN_DIAG:
        sh = int(math.log2(h))
        tb, sb = t >> sh, s >> sh
        masks.append(((tb >> 1) == (sb >> 1)) & ((tb & 1) == 1) & ((sb & 1) == 0))
        h //= 2
    sh = int(math.log2(HGRN_DIAG))
    masks.append(((t >> sh) == (s >> sh)) & (s <= t))
    return masks


def _hgrn_chunk(qf, kk, v, lf, st_ref, masks):
    c, w = lf.shape
    rows = lax.broadcasted_iota(jnp.int32, (c, w), 0)
    b = lf
    sh = 1
    while sh < c:
        b = b + jnp.where(rows >= sh, pltpu.roll(b, sh, axis=0), 0.0)
        sh *= 2
    b_last = b[c - 1:c, :]

    level_q, level_k = [], []
    h = c // 2
    while h >= HGRN_DIAG:
        gdec = jnp.exp(-jnp.abs(b - _block_ref_rows(b, 2 * h, h - 1)))
        level_q.append((qf * gdec).astype(BF16))
        level_k.append((kk * gdec).astype(BF16))
        h //= 2
    dd = b - _block_ref_rows(b, HGRN_DIAG, HGRN_DIAG // 2 - 1)
    level_q.append((qf * jnp.exp(dd)).astype(BF16))
    level_k.append((kk * jnp.exp(-dd)).astype(BF16))

    q_in = (qf * jnp.exp(b)).astype(BF16)
    k_out = (kk * jnp.exp(b_last - b)).astype(BF16)
    vb = v.astype(BF16)
    carry = jnp.exp(b_last)

    outs = []
    dk = HGRN_HEAD_DIM
    for hd in range(w // dk):
        sl = slice(hd * dk, (hd + 1) * dk)
        a = jnp.zeros((c, c), F32)
        for lq, lk, m in zip(level_q, level_k, masks):
            a = jnp.where(m, _dot_nt(lq[:, sl], lk[:, sl]), a)
        st = st_ref[hd]
        o = _dot(a.astype(BF16), vb[:, sl]) + _dot_nt(q_in[:, sl], st.astype(BF16))
        st_ref[hd] = st * carry[:, sl] + _dot_tn(vb[:, sl], k_out[:, sl])
        outs.append(o)
    return jnp.concatenate(outs, axis=-1)


def _mixer_kernel(x_ref, win_ref, cw_ref, cb_ref, cg_ref, cbeta_ref, wcp_ref, lb_ref, ng_ref,
                  whp_ref, wmix_ref, g_ref, b_ref, o_ref,
                  p_scr, kk_scr, o_scr, hbuf, st_scr, *, layer, alpha, cw, hw):
    ts = x_ref.shape[0]
    x = x_ref[...]
    p_scr[...] = _dot(x.astype(BF16), win_ref[...])

    @pl.when(pl.program_id(1) == 0)
    def _():
        hbuf[0:CONV_HALO, :] = jnp.zeros((CONV_HALO, cw), F32)
        st_scr[...] = jnp.zeros(st_scr.shape, F32)

    hbuf[CONV_HALO:CONV_HALO + ts, :] = p_scr[:, 0:cw] * jax.nn.sigmoid(p_scr[:, cw:2 * cw])
    conv = jnp.broadcast_to(cb_ref[...], (ts, cw))
    for j in range(CONV_K):
        start = CONV_HALO - (CONV_K - 1) + j
        conv = conv + cw_ref[j:j + 1, :] * hbuf[start:start + ts, :]
    hbuf[0:CONV_HALO, :] = hbuf[ts:ts + CONV_HALO, :]
    hc = _layer_norm(conv, cg_ref[...], cbeta_ref[...])
    y_conv = _dot((hc * jax.nn.sigmoid(hc)).astype(BF16), wcp_ref[...])

    o0 = 2 * cw
    lb_rows = [lb_ref[i:i + 1, :] for i in range(lb_ref.shape[0])]
    lb_max = functools.reduce(jnp.maximum, lb_rows)
    lb_exp = [jnp.exp(r - lb_max) for r in lb_rows]
    lb_den = functools.reduce(lambda a, c: a + c, lb_exp)
    lb_soft = [e / lb_den for e in lb_exp]
    lb = functools.reduce(lambda a, c: a + c, lb_soft[:layer + 1]) - lb_soft[0]

    q = p_scr[:, o0:o0 + hw]
    z = p_scr[:, o0 + hw:o0 + 2 * hw]
    kk = (1.0 - lb) * jax.nn.sigmoid(-z)
    kk_scr[...] = kk
    p_scr[:, o0 + hw:o0 + 2 * hw] = jnp.log1p(-jnp.minimum(kk, KEY_MAX))
    p_scr[:, o0:o0 + hw] = q * jax.nn.sigmoid(q) * (HGRN_HEAD_DIM ** -0.5)

    masks = _score_masks(HGRN_CHUNK)

    def chunk_body(ci, carry):
        r0 = pl.multiple_of(ci * HGRN_CHUNK, HGRN_CHUNK)
        rs = pl.ds(r0, HGRN_CHUNK)
        o_scr[rs, :] = _hgrn_chunk(p_scr[rs, o0:o0 + hw], kk_scr[rs, :],
                                   p_scr[rs, o0 + 2 * hw:o0 + 3 * hw],
                                   p_scr[rs, o0 + hw:o0 + 2 * hw], st_scr, masks)
        return carry

    lax.fori_loop(0, ts // HGRN_CHUNK, chunk_body, 0)

    o = o_scr[...]
    normed = []
    for hd in range(hw // HGRN_HEAD_DIM):
        oh = o[:, hd * HGRN_HEAD_DIM:(hd + 1) * HGRN_HEAD_DIM]
        normed.append(oh * lax.rsqrt(jnp.mean(oh * oh, axis=-1, keepdims=True) + RMS_EPS))
    og = p_scr[:, o0 + 3 * hw:o0 + 4 * hw]
    oh_all = jnp.concatenate(normed, axis=-1) * ng_ref[...] * (og * jax.nn.sigmoid(og))
    y_hgrn = _dot(oh_all.astype(BF16), whp_ref[...])

    g0 = o0 + 4 * hw
    d = x.shape[-1]
    mix = (jax.nn.sigmoid(p_scr[:, g0:g0 + d]) * y_conv
           + jax.nn.sigmoid(p_scr[:, g0 + d:g0 + 2 * d]) * y_hgrn)
    y = _dot(mix.astype(BF16), wmix_ref[...])
    o_ref[...] = _layer_norm(alpha * x + y, g_ref[...], b_ref[...])


def _mixer_ln(x, win, conv_w, conv_b, cg, cbeta, wcp, lb, ng, whp, wmix, g, b, layer, alpha, ts):
    bsz, seq, d = x.shape
    cw = conv_w.shape[1]
    hw = ng.shape[1]
    assert win.shape[1] == 2 * cw + 4 * hw + 2 * d
    assert ts % HGRN_CHUNK == 0 and seq % ts == 0 and CONV_HALO >= CONV_K - 1
    xspec = pl.BlockSpec((None, ts, d), lambda i, j: (i, j, 0))
    params = (win, conv_w, conv_b, cg, cbeta, wcp, lb, ng, whp, wmix, g, b)
    return pl.pallas_call(
        functools.partial(_mixer_kernel, layer=layer, alpha=alpha, cw=cw, hw=hw),
        grid=(bsz, seq // ts),
        in_specs=[xspec] + [_resident(p.shape) for p in params],
        out_specs=xspec,
        out_shape=jax.ShapeDtypeStruct(x.shape, F32),
        scratch_shapes=[
            pltpu.VMEM((ts, win.shape[1]), F32),
            pltpu.VMEM((ts, hw), F32),
            pltpu.VMEM((ts, hw), F32),
            pltpu.VMEM((CONV_HALO + ts, cw), F32),
            pltpu.VMEM((hw // HGRN_HEAD_DIM, HGRN_HEAD_DIM, HGRN_HEAD_DIM), F32),
        ],
        compiler_params=_cparams(("parallel", "arbitrary")),
        name="mixer_ln",
    )(x, *params)


FFN_ROWS = 512
XATTN_ROWS = 512
MIXER_ROWS = 256


def kernel(x, mem, ln_g, ln_b, ffn1_w_gu, ffn1_w_down, w_in, conv_w, conv_b, conv_ln_g, conv_ln_b, w_conv_proj, hgrn_lb, hgrn_norm_g, w_hgrn_proj, w_mix_out, xa_wq, xa_wkv, xa_wo, ffn2_w_gu, ffn2_w_down):
    bsz, seq, d = x.shape
    depth = ln_g.shape[0]
    alpha = (2 * depth) ** 0.25
    row = lambda a: a[None, :]
    bf = lambda a: a.astype(BF16)

    for l in range(depth):
        x = _ffn_ln(x.reshape(bsz * seq, d), bf(ffn1_w_gu[l]), bf(ffn1_w_down[l]),
                    row(ln_g[l, 0]), row(ln_b[l, 0]), alpha, FFN_ROWS).reshape(bsz, seq, d)
        x = _mixer_ln(x, bf(w_in[l]), conv_w[l], row(conv_b[l]), row(conv_ln_g[l]), row(conv_ln_b[l]),
                      bf(w_conv_proj[l]), hgrn_lb, row(hgrn_norm_g[l]), bf(w_hgrn_proj[l]),
                      bf(w_mix_out[l]), row(ln_g[l, 1]), row(ln_b[l, 1]), l, alpha, MIXER_ROWS)
        k, v = _kv_proj(mem, bf(xa_wkv[l]))
        x = _xattn_ln(x, k, v, bf(xa_wq[l]), bf(xa_wo[l]), row(ln_g[l, 2]), row(ln_b[l, 2]),
                      alpha, XATTN_ROWS)
        x = _ffn_ln(x.reshape(bsz * seq, d), bf(ffn2_w_gu[l]), bf(ffn2_w_down[l]),
                    row(ln_g[l, 3]), row(ln_b[l, 3]), alpha, FFN_ROWS).reshape(bsz, seq, d)
    return x
```

```python
import functools
import math

import jax
import jax.numpy as jnp
from jax import lax
from jax.experimental import pallas as pl
from jax.experimental.pallas import tpu as pltpu

F32 = jnp.float32
BF16 = jnp.bfloat16

LN_EPS = 1e-5
RMS_EPS = 1e-6
KEY_MAX = 1.0 - 1e-6
CONV_K = 31
HGRN_HEAD_DIM = 128
XA_HEADS = 4

V7X_SUBLANES = 8
V7X_LANES = 128
V7X_MXU_COLS = 256
V7X_VMEM_LIMIT_BYTES = 60000 * 1024

CONV_HALO = 32
HGRN_CHUNK = 128
HGRN_DIAG = 8


def _cparams(sem):
    return pltpu.CompilerParams(dimension_semantics=sem, vmem_limit_bytes=V7X_VMEM_LIMIT_BYTES)


def _resident(shape):
    nd = len(shape)
    return pl.BlockSpec(shape, lambda *_: (0,) * nd, pipeline_mode=pl.Buffered(1))


def _layer_norm(y, g, b):
    mu = jnp.mean(y, axis=-1, keepdims=True)
    yc = y - mu
    var = jnp.mean(yc * yc, axis=-1, keepdims=True)
    return yc * lax.rsqrt(var + LN_EPS) * g + b


def _dot(a, b):
    return jnp.dot(a, b, preferred_element_type=F32)


def _dot_nt(a, b):
    return lax.dot_general(a, b, (((1,), (1,)), ((), ())), preferred_element_type=F32)


def _dot_tn(a, b):
    return lax.dot_general(a, b, (((0,), (0,)), ((), ())), preferred_element_type=F32)


def _ffn_kernel(x_ref, wgu_ref, wd_ref, g_ref, b_ref, o_ref, *, d_ff, alpha, col_chunks):
    x = x_ref[...]
    xb = x.astype(BF16)
    starts = [sum(col_chunks[:i]) for i in range(len(col_chunks))]

    def gate_up(i):
        c0, cw = starts[i], col_chunks[i]
        gate = _dot(xb, wgu_ref[:, c0:c0 + cw])
        up = _dot(xb, wgu_ref[:, d_ff + c0:d_ff + c0 + cw])
        return (gate * jax.nn.sigmoid(gate) * up).astype(BF16)

    acc = None
    act = gate_up(0)
    for i in range(len(col_chunks)):
        nxt = gate_up(i + 1) if i + 1 < len(col_chunks) else None
        part = _dot(act, wd_ref[starts[i]:starts[i] + col_chunks[i], :])
        acc = part if acc is None else acc + part
        act = nxt
    o_ref[...] = _layer_norm(alpha * x + 0.5 * acc, g_ref[...], b_ref[...])


def _ffn_col_chunks(d_ff):
    n_tiles = d_ff // V7X_MXU_COLS
    assert n_tiles * V7X_MXU_COLS == d_ff
    per = 3
    out = [per * V7X_MXU_COLS] * (n_tiles // per)
    if n_tiles % per:
        out.append((n_tiles % per) * V7X_MXU_COLS)
    return tuple(out)


def _ffn_ln(x2d, wgu, wd, g, b, alpha, tm):
    t, d = x2d.shape
    d_ff = wd.shape[0]
    kern = functools.partial(_ffn_kernel, d_ff=d_ff, alpha=alpha, col_chunks=_ffn_col_chunks(d_ff))
    return pl.pallas_call(
        kern,
        grid=(t // tm,),
        in_specs=[
            pl.BlockSpec((tm, d), lambda i: (i, 0)),
            _resident(wgu.shape),
            _resident(wd.shape),
            _resident(g.shape),
            _resident(b.shape),
        ],
        out_specs=pl.BlockSpec((tm, d), lambda i: (i, 0)),
        out_shape=jax.ShapeDtypeStruct((t, d), F32),
        compiler_params=_cparams(("parallel",)),
        name="ffn_ln",
    )(x2d, wgu, wd, g, b)


def _kv_kernel(mem_ref, wkv_ref, k_ref, v_ref, *, d):
    kv = _dot(mem_ref[...].astype(BF16), wkv_ref[...])
    k_ref[...] = kv[:, :d].astype(BF16)
    v_ref[...] = kv[:, d:].astype(BF16)


def _kv_proj(mem, wkv):
    bsz, n_mem, d = mem.shape
    spec = pl.BlockSpec((None, n_mem, d), lambda i: (i, 0, 0))
    return pl.pallas_call(
        functools.partial(_kv_kernel, d=d),
        grid=(bsz,),
        in_specs=[spec, _resident(wkv.shape)],
        out_specs=[spec, spec],
        out_shape=[jax.ShapeDtypeStruct((bsz, n_mem, d), BF16)] * 2,
        compiler_params=_cparams(("parallel",)),
        name="kv_proj",
    )(mem, wkv)


def _xattn_kernel(x_ref, k_ref, v_ref, wq_ref, wo_ref, g_ref, b_ref, o_ref, *, alpha, heads):
    x = x_ref[...]
    d = x.shape[-1]
    hd = d // heads
    q = _dot(x.astype(BF16), wq_ref[...])
    qb = q.astype(BF16)
    sls = [slice(h * hd, (h + 1) * hd) for h in range(heads)]

    def probs(h):
        s = _dot_nt(qb[:, sls[h]], k_ref[:, sls[h]]) * (hd ** -0.5)
        e = jnp.exp(s - jnp.max(s, axis=-1, keepdims=True))
        return e.astype(BF16), jnp.sum(e, axis=-1, keepdims=True)

    outs = []
    cur = probs(0)
    for h in range(heads):
        nxt = probs(h + 1) if h + 1 < heads else None
        e, den = cur
        outs.append((_dot(e, v_ref[:, sls[h]]) / den).astype(BF16))
        cur = nxt
    y = _dot(jnp.concatenate(outs, axis=-1), wo_ref[...])
    o_ref[...] = _layer_norm(alpha * x + y, g_ref[...], b_ref[...])


def _xattn_ln(x, k, v, wq, wo, g, b, alpha, tm):
    bsz, seq, d = x.shape
    n_mem = k.shape[1]
    xspec = pl.BlockSpec((None, tm, d), lambda i, j: (i, j, 0))
    kvspec = pl.BlockSpec((None, n_mem, d), lambda i, j: (i, 0, 0))
    return pl.pallas_call(
        functools.partial(_xattn_kernel, alpha=alpha, heads=XA_HEADS),
        grid=(bsz, seq // tm),
        in_specs=[xspec, kvspec, kvspec, _resident(wq.shape), _resident(wo.shape),
                  _resident(g.shape), _resident(b.shape)],
        out_specs=xspec,
        out_shape=jax.ShapeDtypeStruct(x.shape, F32),
        compiler_params=_cparams(("parallel", "parallel")),
        name="xattn_ln",
    )(x, k, v, wq, wo, g, b)


def _block_ref_rows(b, block, row):
    c, w = b.shape
    b3 = b.reshape(c // block, block, w)
    return jnp.broadcast_to(b3[:, row:row + 1, :], b3.shape).reshape(c, w)


def _score_masks(c):
    t = lax.broadcasted_iota(jnp.int32, (c, c), 0)
    s = lax.broadcasted_iota(jnp.int32, (c, c), 1)
    masks = []
    h = c // 2
    while h >= HGRN_DIAG:
        sh = int(math.log2(h))
        tb, sb = t >> sh, s >> sh
        masks.append(((tb >> 1) == (sb >> 1)) & ((tb & 1) == 1) & ((sb & 1) == 0))
        h //= 2
    sh = int(math.log2(HGRN_DIAG))
    masks.append(((t >> sh) == (s >> sh)) & (s <= t))
    return masks


def _hgrn_recurrence(qf, kk, v, lf, st_ref, c):
    t, w = lf.shape
    pos = lax.broadcasted_iota(jnp.int32, (t, w), 0) & (c - 1)
    b = lf
    sh = 1
    while sh < c:
        b = b + jnp.where(pos >= sh, pltpu.roll(b, sh, axis=0), 0.0)
        sh *= 2
    b_last = _block_ref_rows(b, c, c - 1)

    level_q, level_k = [], []
    h = c // 2
    while h >= HGRN_DIAG:
        gdec = jnp.exp(-jnp.abs(b - _block_ref_rows(b, 2 * h, h - 1)))
        level_q.append((qf * gdec).astype(BF16))
        level_k.append((kk * gdec).astype(BF16))
        h //= 2
    dd = b - _block_ref_rows(b, HGRN_DIAG, HGRN_DIAG // 2 - 1)
    level_q.append((qf * jnp.exp(dd)).astype(BF16))
    level_k.append((kk * jnp.exp(-dd)).astype(BF16))

    q_in = (qf * jnp.exp(b)).astype(BF16)
    k_out = (kk * jnp.exp(b_last - b)).astype(BF16)
    vb = v.astype(BF16)
    masks = _score_masks(c)

    dk = HGRN_HEAD_DIM
    heads = w // dk
    states = [st_ref[hd] for hd in range(heads)]
    out_rows = []
    for ci in range(t // c):
        rs = slice(ci * c, (ci + 1) * c)
        carry = jnp.exp(b[(ci + 1) * c - 1:(ci + 1) * c, :])
        outs = []
        for hd in range(heads):
            sl = slice(hd * dk, (hd + 1) * dk)
            a = jnp.zeros((c, c), F32)
            for lq, lk, m in zip(level_q, level_k, masks):
                a = jnp.where(m, _dot_nt(lq[rs, sl], lk[rs, sl]), a)
            st = states[hd]
            outs.append(_dot(a.astype(BF16), vb[rs, sl]) + _dot_nt(q_in[rs, sl], st.astype(BF16)))
            states[hd] = st * carry[:, sl] + _dot_tn(vb[rs, sl], k_out[rs, sl])
        out_rows.append(jnp.concatenate(outs, axis=-1))
    for hd in range(heads):
        st_ref[hd] = states[hd]
    return jnp.concatenate(out_rows, axis=0)


def _causal_depthwise_conv(hbuf, w_ref, b_ref, ts):
    slabs = []
    for c in range(hbuf.shape[0]):
        ls = slice(c * V7X_LANES, (c + 1) * V7X_LANES)
        acc = jnp.broadcast_to(b_ref[:, ls], (ts, V7X_LANES))
        for j in range(CONV_K):
            start = CONV_HALO - (CONV_K - 1) + j
            acc = acc + w_ref[j:j + 1, ls] * hbuf[c, start:start + ts, :]
        slabs.append(acc)
    return jnp.concatenate(slabs, axis=-1)


def _mixer_kernel(x_ref, win_ref, cw_ref, cb_ref, cg_ref, cbeta_ref, wcp_ref, lb_ref, ng_ref,
                  whp_ref, wmix_ref, g_ref, b_ref, o_ref, hbuf, st_scr, *, layer, alpha, cw, hw):
    ts = x_ref.shape[0]
    x = x_ref[...]
    xb = x.astype(BF16)
    d = x.shape[-1]

    @pl.when(pl.program_id(1) == 0)
    def _():
        hbuf[:, 0:CONV_HALO, :] = jnp.zeros((hbuf.shape[0], CONV_HALO, V7X_LANES), F32)
        st_scr[...] = jnp.zeros(st_scr.shape, F32)

    o0 = 2 * cw
    g0 = o0 + 4 * hw
    pc = _dot(xb, win_ref[:, 0:o0])
    ph = _dot(xb, win_ref[:, o0:g0])
    pg = _dot(xb, win_ref[:, g0:g0 + 2 * d])

    glu = pc[:, 0:cw] * jax.nn.sigmoid(pc[:, cw:2 * cw])
    for c in range(hbuf.shape[0]):
        hbuf[c, CONV_HALO:CONV_HALO + ts, :] = glu[:, c * V7X_LANES:(c + 1) * V7X_LANES]
    conv = _causal_depthwise_conv(hbuf, cw_ref, cb_ref, ts)
    hbuf[:, 0:CONV_HALO, :] = hbuf[:, ts:ts + CONV_HALO, :]
    hc = _layer_norm(conv, cg_ref[...], cbeta_ref[...])
    hcs = (hc * jax.nn.sigmoid(hc)).astype(BF16)

    lb_rows = [lb_ref[i:i + 1, :] for i in range(lb_ref.shape[0])]
    lb_max = functools.reduce(jnp.maximum, lb_rows)
    lb_exp = [jnp.exp(r - lb_max) for r in lb_rows]
    lb_den = functools.reduce(lambda a, c: a + c, lb_exp)
    lb_soft = [e / lb_den for e in lb_exp]
    lb = functools.reduce(lambda a, c: a + c, lb_soft[:layer + 1]) - lb_soft[0]

    q = ph[:, 0:hw]
    kk = (1.0 - lb) * jax.nn.sigmoid(-ph[:, hw:2 * hw])
    lf = jnp.log1p(-jnp.minimum(kk, KEY_MAX))
    qf = q * jax.nn.sigmoid(q) * (HGRN_HEAD_DIM ** -0.5)
    o = _hgrn_recurrence(qf, kk, ph[:, 2 * hw:3 * hw], lf, st_scr, HGRN_CHUNK)

    y_conv = _dot(hcs, wcp_ref[...])
    normed = []
    for hd in range(hw // HGRN_HEAD_DIM):
        oh = o[:, hd * HGRN_HEAD_DIM:(hd + 1) * HGRN_HEAD_DIM]
        normed.append(oh * lax.rsqrt(jnp.mean(oh * oh, axis=-1, keepdims=True) + RMS_EPS))
    og = ph[:, 3 * hw:4 * hw]
    oh_all = jnp.concatenate(normed, axis=-1) * ng_ref[...] * (og * jax.nn.sigmoid(og))
    y_hgrn = _dot(oh_all.astype(BF16), whp_ref[...])

    mix = jax.nn.sigmoid(pg[:, 0:d]) * y_conv + jax.nn.sigmoid(pg[:, d:2 * d]) * y_hgrn
    y = _dot(mix.astype(BF16), wmix_ref[...])
    o_ref[...] = _layer_norm(alpha * x + y, g_ref[...], b_ref[...])


def _mixer_ln(x, win, conv_w, conv_b, cg, cbeta, wcp, lb, ng, whp, wmix, g, b, layer, alpha, ts):
    bsz, seq, d = x.shape
    cw = conv_w.shape[1]
    hw = ng.shape[1]
    assert win.shape[1] == 2 * cw + 4 * hw + 2 * d
    assert ts % HGRN_CHUNK == 0 and seq % ts == 0 and CONV_HALO >= CONV_K - 1
    xspec = pl.BlockSpec((None, ts, d), lambda i, j: (i, j, 0))
    params = (win, conv_w, conv_b, cg, cbeta, wcp, lb, ng, whp, wmix, g, b)
    return pl.pallas_call(
        functools.partial(_mixer_kernel, layer=layer, alpha=alpha, cw=cw, hw=hw),
        grid=(bsz, seq // ts),
        in_specs=[xspec] + [_resident(p.shape) for p in params],
        out_specs=xspec,
        out_shape=jax.ShapeDtypeStruct(x.shape, F32),
        scratch_shapes=[
            pltpu.VMEM((cw // V7X_LANES, CONV_HALO + ts, V7X_LANES), F32),
            pltpu.VMEM((hw // HGRN_HEAD_DIM, HGRN_HEAD_DIM, HGRN_HEAD_DIM), F32),
        ],
        compiler_params=_cparams(("parallel", "arbitrary")),
        name="mixer_ln",
    )(x, *params)


FFN_ROWS = 1024
XATTN_ROWS = 1024
MIXER_ROWS = 512


def kernel(x, mem, ln_g, ln_b, ffn1_w_gu, ffn1_w_down, w_in, conv_w, conv_b, conv_ln_g, conv_ln_b, w_conv_proj, hgrn_lb, hgrn_norm_g, w_hgrn_proj, w_mix_out, xa_wq, xa_wkv, xa_wo, ffn2_w_gu, ffn2_w_down):
    bsz, seq, d = x.shape
    depth = ln_g.shape[0]
    alpha = (2 * depth) ** 0.25
    row = lambda a: a[None, :]
    bf = lambda a: a.astype(BF16)

    for l in range(depth):
        x = _ffn_ln(x.reshape(bsz * seq, d), bf(ffn1_w_gu[l]), bf(ffn1_w_down[l]),
                    row(ln_g[l, 0]), row(ln_b[l, 0]), alpha, FFN_ROWS).reshape(bsz, seq, d)
        x = _mixer_ln(x, bf(w_in[l]), conv_w[l], row(conv_b[l]), row(conv_ln_g[l]), row(conv_ln_b[l]),
                      bf(w_conv_proj[l]), hgrn_lb, row(hgrn_norm_g[l]), bf(w_hgrn_proj[l]),
                      bf(w_mix_out[l]), row(ln_g[l, 1]), row(ln_b[l, 1]), l, alpha, MIXER_ROWS)
        k, v = _kv_proj(mem, bf(xa_wkv[l]))
        x = _xattn_ln(x, k, v, bf(xa_wq[l]), bf(xa_wo[l]), row(ln_g[l, 2]), row(ln_b[l, 2]),
                      alpha, XATTN_ROWS)
        x = _ffn_ln(x.reshape(bsz * seq, d), bf(ffn2_w_gu[l]), bf(ffn2_w_down[l]),
                    row(ln_g[l, 3]), row(ln_b[l, 3]), alpha, FFN_ROWS).reshape(bsz, seq, d)
    return x
```
